```python
import math
import jax, jax.numpy as jnp
from jax import lax
import numpy as np

D_MODEL = 4096
BATCH = 4
SEQ = 2048
DEPTH = 2
DEC_BATCH = 2
DEC_SEQ = 4096
PAST_LEN = 128

HEAD_DIM = 128
GRID_W = 64
ROPE_THETA = 10000.0
EPS = 1e-6
Q_BLOCK = 128
A_HEADS = D_MODEL // 256
A_KV_HEADS = A_HEADS // 4
B_HEADS = D_MODEL // 512
C_HEADS = D_MODEL // 256
C_GROUPS = ((128, 1), (512, 4), (2048, 16))
D_FF = 256 * ((8 * D_MODEL // 3 + 255) // 256)
CONV_W = 3

kernel_name = "hybrid_bidir_encoder_gqa_diffattn_dilated_convffn"


def rms_norm(x, g):
    xf = x.astype(jnp.float32)
    y = xf * lax.rsqrt(jnp.mean(xf * xf, axis=-1, keepdims=True) + EPS)
    return (y * g.astype(jnp.float32)).astype(x.dtype)


def alibi_slopes(n):
    return 2.0 ** (-8.0 * jnp.arange(1, n + 1, dtype=jnp.float32) / n)


def query_blocks(q):
    b, t = q.shape[:2]
    return jnp.moveaxis(q.reshape(b, t // Q_BLOCK, Q_BLOCK, *q.shape[2:]), 1, 0)


def merge_blocks(o):
    o = jnp.moveaxis(o, 0, 1)
    return o.reshape(o.shape[0], -1, *o.shape[3:])


def axial_rope_tables(t):
    rows = t // GRID_W
    row = jnp.repeat(jnp.arange(rows, dtype=jnp.float32), GRID_W)
    col = jnp.tile(jnp.arange(GRID_W, dtype=jnp.float32), rows)
    axis_dim = HEAD_DIM // 2
    inv = ROPE_THETA ** (-jnp.arange(0, axis_dim, 2, dtype=jnp.float32) / axis_dim)
    ang_r = row[:, None] * inv[None, :]
    ang_c = col[:, None] * inv[None, :]
    return jnp.cos(ang_r), jnp.sin(ang_r), jnp.cos(ang_c), jnp.sin(ang_c)


def rotate(x, cos, sin):
    xf = x.astype(jnp.float32)
    x1, x2 = jnp.split(xf, 2, axis=-1)
    c = cos[:, None, :]
    s = sin[:, None, :]
    return jnp.concatenate([x1 * c - x2 * s, x1 * s + x2 * c], axis=-1).astype(x.dtype)


def apply_axial_rope(x, tabs):
    cr, sr, cc, sc = tabs
    xr, xc = jnp.split(x, 2, axis=-1)
    return jnp.concatenate([rotate(xr, cr, sr), rotate(xc, cc, sc)], axis=-1)


def gqa_axial_attention(q, k, v, q_gain, k_gain):
    b, t = q.shape[:2]
    tabs = axial_rope_tables(t)
    q = apply_axial_rope(rms_norm(q, q_gain), tabs)
    k = apply_axial_rope(rms_norm(k, k_gain), tabs)
    grp = A_HEADS // A_KV_HEADS
    q = q.reshape(b, t, A_KV_HEADS, grp, HEAD_DIM)
    scale = HEAD_DIM ** -0.5

    def block(qb):
        s = jnp.einsum('bqkgd,bskd->bkgqs', qb, k).astype(jnp.float32) * scale
        p = jax.nn.softmax(s, axis=-1).astype(v.dtype)
        return jnp.einsum('bkgqs,bskd->bqkgd', p, v)

    o = merge_blocks(lax.map(block, query_blocks(q)))
    return o.reshape(b, t, A_HEADS * HEAD_DIM)


def diff_attention(q, k, v, lam_q1, lam_k1, lam_q2, lam_k2, sub_gain, lambda_init):
    b, t = q.shape[:2]
    slopes = alibi_slopes(B_HEADS)
    pos = jnp.arange(t, dtype=jnp.float32)
    lam = (jnp.exp(jnp.sum(lam_q1.astype(jnp.float32) * lam_k1.astype(jnp.float32)))
           - jnp.exp(jnp.sum(lam_q2.astype(jnp.float32) * lam_k2.astype(jnp.float32)))
           + lambda_init)
    scale = HEAD_DIM ** -0.5

    def block(args):
        qb, qpos = args
        dist = jnp.abs(qpos[:, None] - pos[None, :])
        bias = -slopes[:, None, None] * dist[None]
        s = jnp.einsum('bqhcd,bshcd->bhcqs', qb, k).astype(jnp.float32) * scale + bias[None, :, None]
        p = jax.nn.softmax(s, axis=-1)
        a = p[:, :, 0] - lam * p[:, :, 1]
        return jnp.einsum('bhqs,bshe->bqhe', a.astype(v.dtype), v)

    qpos = pos.reshape(t // Q_BLOCK, Q_BLOCK)
    o = merge_blocks(lax.map(block, (query_blocks(q), qpos)))
    o = rms_norm(o, sub_gain) * (1.0 - lambda_init)
    return o.reshape(b, t, B_HEADS * 2 * HEAD_DIM)


def dilated_group(q, k, v, window, r, slopes):
    b, t, h, dh = q.shape
    rad = window // (2 * r)
    blk = rad
    L = t // r
    nb = -(-L // blk)
    lp = nb * blk

    def to_residue(x):
        x = x.reshape(b, L, r, h, dh).transpose(0, 2, 1, 3, 4)
        return jnp.pad(x, ((0, 0), (0, 0), (0, lp - L), (0, 0), (0, 0)))

    def band(x):
        xp = jnp.pad(x, ((0, 0), (0, 0), (blk, blk), (0, 0), (0, 0)))
        xb = xp.reshape(b, r, nb + 2, blk, h, dh)
        return jnp.concatenate([xb[:, :, :-2], xb[:, :, 1:-1], xb[:, :, 2:]], axis=3)

    qr = to_residue(q).reshape(b, r, nb, blk, h, dh)
    kb = band(to_residue(k))
    vb = band(to_residue(v))
    qi = jnp.arange(lp).reshape(nb, blk)
    ki = jnp.arange(nb)[:, None] * blk - blk + jnp.arange(3 * blk)[None, :]
    rel = jnp.abs(qi[:, :, None] - ki[:, None, :])
    valid = (rel <= rad) & (ki[:, None, :] >= 0) & (ki[:, None, :] < L)
    bias = -slopes[:, None, None, None] * (r * rel).astype(jnp.float32)[None]
    scale = HEAD_DIM ** -0.5
    s = jnp.einsum('bcnqhd,bcnkhd->bchnqk', qr, kb).astype(jnp.float32) * scale + bias[None, None]
    s = jnp.where(valid[None, None, None], s, -1e30)
    m = jnp.max(s, axis=-1, keepdims=True)
    e = jnp.exp(s - m)
    den = jnp.sum(e, axis=-1, keepdims=True)
    p = (e / den).astype(v.dtype)
    lse = (m + jnp.log(den))[..., 0]
    o = jnp.einsum('bchnqk,bcnkhd->bcnqhd', p, vb).reshape(b, r, lp, h, dh)[:, :, :L]
    o = o.transpose(0, 2, 1, 3, 4).reshape(b, t, h, dh)
    lse = lse.transpose(0, 1, 3, 4, 2).reshape(b, r, lp, h)[:, :, :L]
    lse = lse.transpose(0, 2, 1, 3).reshape(b, t, h)
    return o, lse


def mixer_ab(h, w_in, q_gain, k_gain, lam_q1, lam_k1, lam_q2, lam_k2, subln, w_out, lambda_init):
    b, t, _ = h.shape
    aq = A_HEADS * HEAD_DIM
    akv = A_KV_HEADS * HEAD_DIM
    bw = B_HEADS * 2 * HEAD_DIM
    cuts = [aq, aq + akv, aq + 2 * akv, aq + 2 * akv + bw, aq + 2 * akv + 2 * bw]
    proj = h @ w_in
    qa, ka, va, qb, kb, vb = jnp.split(proj, cuts, axis=-1)
    oa = gqa_axial_attention(qa.reshape(b, t, A_HEADS, HEAD_DIM),
                             ka.reshape(b, t, A_KV_HEADS, HEAD_DIM),
                             va.reshape(b, t, A_KV_HEADS, HEAD_DIM), q_gain, k_gain)
    ob = diff_attention(qb.reshape(b, t, B_HEADS, 2, HEAD_DIM),
                        kb.reshape(b, t, B_HEADS, 2, HEAD_DIM),
                        vb.reshape(b, t, B_HEADS, 2 * HEAD_DIM),
                        lam_q1, lam_k1, lam_q2, lam_k2, subln, lambda_init)
    return jnp.concatenate([oa, ob], axis=-1) @ w_out


def mixer_c(h, w_in, w_out):
    b, t, _ = h.shape
    proj = (h @ w_in).reshape(b, t, len(C_GROUPS), 3, C_HEADS, HEAD_DIM)
    slopes = alibi_slopes(C_HEADS)
    outs, lses = [], []
    for g, (window, r) in enumerate(C_GROUPS):
        o, lse = dilated_group(proj[:, :, g, 0], proj[:, :, g, 1], proj[:, :, g, 2], window, r, slopes)
        outs.append(o)
        lses.append(lse)
    alpha = jax.nn.softmax(jnp.stack(lses, axis=0), axis=0)
    o = jnp.sum(alpha[..., None].astype(h.dtype) * jnp.stack(outs, axis=0), axis=0)
    return o.reshape(b, t, C_HEADS * HEAD_DIM) @ w_out


def conv_ffn(x, w_up, conv_w, conv_b, w_down):
    t = x.shape[1]
    hu = x @ w_up
    pad = CONV_W // 2
    hp = jnp.pad(hu, ((0, 0), (pad, pad), (0, 0)))
    hc = sum(hp[:, j:j + t] * conv_w[j] for j in range(CONV_W)) + conv_b
    g, u = jnp.split(hc, 2, axis=-1)
    return (jax.nn.silu(g) * u) @ w_down


def setup_inputs(seed: int = 0) -> dict:
    key = jax.random.key(seed)
    ks = iter(jax.random.split(key, 40))

    def nrm(shape, scale):
        return jax.random.normal(next(ks), shape, jnp.float32) * scale

    def gain(n):
        return 1.0 + 0.01 * jax.random.normal(next(ks), (n,), jnp.float32)

    in0 = A_HEADS * HEAD_DIM + 2 * A_KV_HEADS * HEAD_DIM + 3 * B_HEADS * 2 * HEAD_DIM
    out0 = A_HEADS * HEAD_DIM + B_HEADS * 2 * HEAD_DIM
    in1 = len(C_GROUPS) * 3 * C_HEADS * HEAD_DIM
    out1 = C_HEADS * HEAD_DIM
    d = D_MODEL
    inp = {}
    inp["x_prompt"] = nrm((BATCH, SEQ, d), 1.0)
    inp["x_sample"] = nrm((DEC_BATCH, DEC_SEQ, d), 1.0)
    inp["l0_norm_pre_mix"] = gain(d)
    inp["l0_w_in"] = nrm((d, in0), d ** -0.5)
    inp["l0_q_norm"] = gain(HEAD_DIM)
    inp["l0_k_norm"] = gain(HEAD_DIM)
    inp["l0_lam_q1"] = nrm((HEAD_DIM,), 0.1)
    inp["l0_lam_k1"] = nrm((HEAD_DIM,), 0.1)
    inp["l0_lam_q2"] = nrm((HEAD_DIM,), 0.1)
    inp["l0_lam_k2"] = nrm((HEAD_DIM,), 0.1)
    inp["l0_subln"] = gain(2 * HEAD_DIM)
    inp["l0_w_out"] = nrm((out0, d), out0 ** -0.5)
    inp["l0_norm_post_mix"] = gain(d)
    inp["l0_norm_pre_ffn"] = gain(d)
    inp["l0_ffn_w_up"] = nrm((d, 2 * D_FF), d ** -0.5)
    inp["l0_ffn_conv_w"] = nrm((CONV_W, 2 * D_FF), CONV_W ** -0.5)
    inp["l0_ffn_conv_b"] = nrm((2 * D_FF,), 0.01)
    inp["l0_ffn_w_down"] = nrm((D_FF, d), D_FF ** -0.5)
    inp["l0_norm_post_ffn"] = gain(d)
    inp["l1_norm_pre_mix"] = gain(d)
    inp["l1_w_in"] = nrm((d, in1), d ** -0.5)
    inp["l1_w_out"] = nrm((out1, d), out1 ** -0.5)
    inp["l1_norm_post_mix"] = gain(d)
    inp["l1_norm_pre_ffn"] = gain(d)
    inp["l1_ffn_w_up"] = nrm((d, 2 * D_FF), d ** -0.5)
    inp["l1_ffn_conv_w"] = nrm((CONV_W, 2 * D_FF), CONV_W ** -0.5)
    inp["l1_ffn_conv_b"] = nrm((2 * D_FF,), 0.01)
    inp["l1_ffn_w_down"] = nrm((D_FF, d), D_FF ** -0.5)
    inp["l1_norm_post_ffn"] = gain(d)
    return inp


def reference(x_prompt, x_sample,
              l0_norm_pre_mix, l0_w_in, l0_q_norm, l0_k_norm, l0_lam_q1, l0_lam_k1, l0_lam_q2, l0_lam_k2,
              l0_subln, l0_w_out, l0_norm_post_mix, l0_norm_pre_ffn, l0_ffn_w_up, l0_ffn_conv_w,
              l0_ffn_conv_b, l0_ffn_w_down, l0_norm_post_ffn,
              l1_norm_pre_mix, l1_w_in, l1_w_out, l1_norm_post_mix, l1_norm_pre_ffn, l1_ffn_w_up,
              l1_ffn_conv_w, l1_ffn_conv_b, l1_ffn_w_down, l1_norm_post_ffn):
    norms = [(l0_norm_pre_mix, l0_norm_post_mix, l0_norm_pre_ffn, l0_norm_post_ffn),
             (l1_norm_pre_mix, l1_norm_post_mix, l1_norm_pre_ffn, l1_norm_post_ffn)]
    even_params = [(l0_w_in, l0_q_norm, l0_k_norm, l0_lam_q1, l0_lam_k1, l0_lam_q2, l0_lam_k2,
                    l0_subln, l0_w_out)]
    odd_params = [(l1_w_in, l1_w_out)]
    ffns = [(l0_ffn_w_up, l0_ffn_conv_w, l0_ffn_conv_b, l0_ffn_w_down),
            (l1_ffn_w_up, l1_ffn_conv_w, l1_ffn_conv_b, l1_ffn_w_down)]

    def trunk(x):
        for layer in range(DEPTH):
            pre_m, post_m, pre_f, post_f = norms[layer]
            h = rms_norm(x, pre_m)
            if layer % 2 == 0:
                lambda_init = 0.8 - 0.6 * math.exp(-0.3 * layer)
                h = mixer_ab(h, *even_params[layer // 2], lambda_init=lambda_init)
            else:
                h = mixer_c(h, *odd_params[layer // 2])
            x = x + rms_norm(h, post_m)
            x = x + rms_norm(conv_ffn(rms_norm(x, pre_f), *ffns[layer]), post_f)
        return x

    y_prompt = trunk(x_prompt)
    y_sample = trunk(x_sample)
    return (y_prompt, y_sample)
```

```python
import functools
import math

import jax
import jax.numpy as jnp
from jax import lax
from jax.experimental import pallas as pl
from jax.experimental.pallas import tpu as pltpu

F32 = jnp.float32
BF16 = jnp.bfloat16

HEAD_DIM = 128
GRID_W = 64
ROPE_THETA = 10000.0
EPS = 1e-6
A_HEADS = 16
A_KV_HEADS = 4
B_HEADS = 8
C_HEADS = 16
C_GROUPS = ((128, 1), (512, 4), (2048, 16))
CONV_W = 3
SCALE = HEAD_DIM ** -0.5
MASK_VALUE = -1e30

V7X_VMEM_BYTES = 64 * 1024 * 1024
VMEM_LIMIT_BYTES = V7X_VMEM_BYTES - 8 * 1024 * 1024


def _cparams(*sem):
    return pltpu.CompilerParams(dimension_semantics=sem, vmem_limit_bytes=VMEM_LIMIT_BYTES)


def _tile(dim, pref):
    return pref if dim % pref == 0 else dim


def _rmsnorm_kernel(x_ref, g_ref, o_ref):
    x = x_ref[...]
    y = x * lax.rsqrt(jnp.mean(x * x, axis=-1, keepdims=True) + EPS)
    o_ref[...] = (y * g_ref[...]).astype(o_ref.dtype)


def rmsnorm_bf16(x, gain):
    m, d = x.shape
    tm = _tile(m, 256)
    return pl.pallas_call(
        _rmsnorm_kernel,
        grid=(m // tm,),
        in_specs=[pl.BlockSpec((tm, d), lambda i: (i, 0)),
                  pl.BlockSpec((1, d), lambda i: (0, 0))],
        out_specs=pl.BlockSpec((tm, d), lambda i: (i, 0)),
        out_shape=jax.ShapeDtypeStruct((m, d), BF16),
        compiler_params=_cparams("parallel"),
        name="rmsnorm",
    )(x, gain.reshape(1, d))


def _post_kernel(h_ref, x_ref, gp_ref, gn_ref, xo_ref, xn_ref):
    h = h_ref[...].astype(F32)
    y = h * lax.rsqrt(jnp.mean(h * h, axis=-1, keepdims=True) + EPS) * gp_ref[...]
    xo = x_ref[...] + y
    xo_ref[...] = xo
    z = xo * lax.rsqrt(jnp.mean(xo * xo, axis=-1, keepdims=True) + EPS)
    xn_ref[...] = (z * gn_ref[...]).astype(xn_ref.dtype)


def _post_last_kernel(h_ref, x_ref, gp_ref, xo_ref):
    h = h_ref[...].astype(F32)
    y = h * lax.rsqrt(jnp.mean(h * h, axis=-1, keepdims=True) + EPS) * gp_ref[...]
    xo_ref[...] = x_ref[...] + y


def post_norm_residual(h, x, g_post, g_next=None):
    m, d = x.shape
    tm = _tile(m, 256)
    row = pl.BlockSpec((tm, d), lambda i: (i, 0))
    vec = pl.BlockSpec((1, d), lambda i: (0, 0))
    if g_next is None:
        return pl.pallas_call(
            _post_last_kernel, grid=(m // tm,), in_specs=[row, row, vec], out_specs=row,
            out_shape=jax.ShapeDtypeStruct((m, d), F32),
            compiler_params=_cparams("parallel"), name="post_last",
        )(h, x, g_post.reshape(1, d))
    return pl.pallas_call(
        _post_kernel, grid=(m // tm,), in_specs=[row, row, vec, vec], out_specs=[row, row],
        out_shape=[jax.ShapeDtypeStruct((m, d), F32), jax.ShapeDtypeStruct((m, d), BF16)],
        compiler_params=_cparams("parallel"), name="post",
    )(h, x, g_post.reshape(1, d), g_next.reshape(1, d))


def _mm_kernel(a_ref, w_ref, o_ref):
    o_ref[...] = jnp.dot(a_ref[...], w_ref[...], preferred_element_type=F32).astype(o_ref.dtype)


def matmul(a, w, tm, tn):
    m, k = a.shape
    n = w.shape[1]
    tm = _tile(m, tm)
    tn = _tile(n, tn)
    return pl.pallas_call(
        _mm_kernel,
        grid=(m // tm, n // tn),
        in_specs=[pl.BlockSpec((tm, k), lambda i, j: (i, 0)),
                  pl.BlockSpec((k, tn), lambda i, j: (0, j))],
        out_specs=pl.BlockSpec((tm, tn), lambda i, j: (i, j)),
        out_shape=jax.ShapeDtypeStruct((m, n), BF16),
        compiler_params=_cparams("parallel", "arbitrary"),
        name="matmul",
    )(a, w)


def _mm2_kernel(a1_ref, a2_ref, w1_ref, w2_ref, o_ref):
    acc = jnp.dot(a1_ref[...], w1_ref[...], preferred_element_type=F32)
    acc = acc + jnp.dot(a2_ref[...], w2_ref[...], preferred_element_type=F32)
    o_ref[...] = acc.astype(o_ref.dtype)


def matmul2(a1, a2, w1, w2, tm, tn):
    m, k1 = a1.shape
    k2 = a2.shape[1]
    n = w1.shape[1]
    tm = _tile(m, tm)
    tn = _tile(n, tn)
    return pl.pallas_call(
        _mm2_kernel,
        grid=(m // tm, n // tn),
        in_specs=[pl.BlockSpec((tm, k1), lambda i, j: (i, 0)),
                  pl.BlockSpec((tm, k2), lambda i, j: (i, 0)),
                  pl.BlockSpec((k1, tn), lambda i, j: (0, j)),
                  pl.BlockSpec((k2, tn), lambda i, j: (0, j))],
        out_specs=pl.BlockSpec((tm, tn), lambda i, j: (i, j)),
        out_shape=jax.ShapeDtypeStruct((m, n), BF16),
        compiler_params=_cparams("parallel", "arbitrary"),
        name="matmul2",
    )(a1, a2, w1, w2)


def _axial_tables(t):
    rows = t // GRID_W
    row = jnp.repeat(jnp.arange(rows, dtype=F32), GRID_W)
    col = jnp.tile(jnp.arange(GRID_W, dtype=F32), rows)
    axis_dim = HEAD_DIM // 2
    inv = ROPE_THETA ** (-jnp.arange(0, axis_dim, 2, dtype=F32) / axis_dim)
    ang_r = row[:, None] * inv[None, :]
    ang_c = col[:, None] * inv[None, :]
    cos = jnp.concatenate([jnp.cos(ang_r)] * 2 + [jnp.cos(ang_c)] * 2, axis=-1)
    sin = jnp.concatenate([-jnp.sin(ang_r), jnp.sin(ang_r), -jnp.sin(ang_c), jnp.sin(ang_c)], axis=-1)
    return cos, sin


def _qk_prep_kernel(p_ref, cos_ref, sin_ref, qg_ref, kg_ref, o_ref, *, n_q_blocks, heads_per_block):
    j = pl.program_id(2)
    is_q = j < n_q_blocks
    gain = jnp.where(is_q, qg_ref[...], kg_ref[...])
    out_scale = jnp.where(is_q, SCALE, 1.0).astype(F32)
    cos = cos_ref[...]
    sin = sin_ref[...]
    lane = lax.broadcasted_iota(jnp.int32, cos.shape, 1)
    first_half = (lane % (HEAD_DIM // 2)) < (HEAD_DIM // 4)
    for h in range(heads_per_block):
        cols = slice(h * HEAD_DIM, (h + 1) * HEAD_DIM)
        x = p_ref[:, cols].astype(F32)
        y = x * lax.rsqrt(jnp.mean(x * x, axis=-1, keepdims=True) + EPS) * gain
        partner = jnp.where(first_half,
                            pltpu.roll(y, HEAD_DIM - HEAD_DIM // 4, 1),
                            pltpu.roll(y, HEAD_DIM // 4, 1))
        o_ref[:, cols] = ((y * cos + partner * sin) * out_scale).astype(o_ref.dtype)


def qk_prep(proj, q_gain, k_gain):
    b, t, _ = proj.shape
    hpb = 4
    width = hpb * HEAD_DIM
    n_q_blocks = A_HEADS // hpb
    n_blocks = (A_HEADS + A_KV_HEADS) // hpb
    tt = _tile(t, 512)
    cos, sin = _axial_tables(t)
    tab = pl.BlockSpec((tt, HEAD_DIM), lambda bi, i, j: (i, 0))
    vec = pl.BlockSpec((1, HEAD_DIM), lambda bi, i, j: (0, 0))
    blk = pl.BlockSpec((None, tt, width), lambda bi, i, j: (bi, i, j))
    return pl.pallas_call(
        functools.partial(_qk_prep_kernel, n_q_blocks=n_q_blocks, heads_per_block=hpb),
        grid=(b, t // tt, n_blocks),
        in_specs=[blk, tab, tab, vec, vec],
        out_specs=blk,
        out_shape=jax.ShapeDtypeStruct((b, t, n_blocks * width), BF16),
        compiler_params=_cparams("parallel", "parallel", "arbitrary"),
        name="qk_prep",
    )(proj, cos, sin, q_gain.reshape(1, HEAD_DIM), k_gain.reshape(1, HEAD_DIM))


def _attn_a_kernel(q_ref, k_ref, v_ref, o_ref, *, tq, tk, n_kv, grp):
    q = jnp.concatenate([q_ref[:, g * HEAD_DIM:(g + 1) * HEAD_DIM] for g in range(grp)], axis=0)
    rows = grp * tq

    def body(j, carry):
        m, l, acc = carry
        ks = pl.multiple_of(j * tk, tk)
        k = k_ref[pl.ds(ks, tk), :]
        v = v_ref[pl.ds(ks, tk), :]
        s = lax.dot_general(q, k, (((1,), (1,)), ((), ())), preferred_element_type=F32)
        m_new = jnp.maximum(m, jnp.max(s, axis=-1, keepdims=True))
        alpha = jnp.exp(m - m_new)
        p = jnp.exp(s - m_new)
        l = alpha * l + jnp.sum(p, axis=-1, keepdims=True)
        acc = alpha * acc + jnp.dot(p.astype(BF16), v, preferred_element_type=F32)
        return m_new, l, acc

    init = (jnp.full((rows, 1), MASK_VALUE, F32), jnp.zeros((rows, 1), F32),
            jnp.zeros((rows, HEAD_DIM), F32))
    _, l, acc = lax.fori_loop(0, n_kv, body, init)
    o = acc / l
    for g in range(grp):
        o_ref[:, g * HEAD_DIM:(g + 1) * HEAD_DIM] = o[g * tq:(g + 1) * tq].astype(o_ref.dtype)


def attention_a(qk, proj, v_col_block):
    b, t, _ = qk.shape
    grp = A_HEADS // A_KV_HEADS
    tq = _tile(t, 128)
    tk = _tile(t, 512)
    return pl.pallas_call(
        functools.partial(_attn_a_kernel, tq=tq, tk=tk, n_kv=t // tk, grp=grp),
        grid=(b, A_KV_HEADS, t // tq),
        in_specs=[pl.BlockSpec((None, tq, grp * HEAD_DIM), lambda bi, h, i: (bi, i, h)),
                  pl.BlockSpec((None, t, HEAD_DIM), lambda bi, h, i: (bi, 0, A_HEADS + h)),
                  pl.BlockSpec((None, t, HEAD_DIM), lambda bi, h, i: (bi, 0, v_col_block + h))],
        out_specs=pl.BlockSpec((None, tq, grp * HEAD_DIM), lambda bi, h, i: (bi, i, h)),
        out_shape=jax.ShapeDtypeStruct((b, t, A_HEADS * HEAD_DIM), BF16),
        compiler_params=_cparams("parallel", "parallel", "arbitrary"),
        name="attn_a",
    )(qk, qk, proj)


def _alibi_slopes(n):
    return 2.0 ** (-8.0 * jnp.arange(1, n + 1, dtype=F32) / n)


def _attn_b_kernel(slopes_ref, lam_ref, sub_ref, q_ref, k_ref, v_ref, o_ref, *, tq, tk, n_kv, lambda_init):
    h = pl.program_id(1)
    qi = pl.program_id(2)
    slope = slopes_ref[h]
    lam_v = lam_ref[...]
    lam = (jnp.exp(jnp.sum(lam_v[0:1] * lam_v[1:2], axis=-1, keepdims=True))
           - jnp.exp(jnp.sum(lam_v[2:3] * lam_v[3:4], axis=-1, keepdims=True)) + lambda_init)
    q0 = q_ref[:, :HEAD_DIM]
    q1 = q_ref[:, HEAD_DIM:]
    rel0 = (lax.broadcasted_iota(jnp.int32, (tq, tk), 0)
            - lax.broadcasted_iota(jnp.int32, (tq, tk), 1)).astype(F32)
    dn = (((1,), (1,)), ((), ()))

    def online(s, v, m, l, acc):
        m_new = jnp.maximum(m, jnp.max(s, axis=-1, keepdims=True))
        alpha = jnp.exp(m - m_new)
        p = jnp.exp(s - m_new)
        l = alpha * l + jnp.sum(p, axis=-1, keepdims=True)
        acc = alpha * acc + jnp.dot(p.astype(BF16), v, preferred_element_type=F32)
        return m_new, l, acc

    def body(j, carry):
        m0, l0, a0, m1, l1, a1 = carry
        ks = pl.multiple_of(j * tk, tk)
        k0 = k_ref[pl.ds(ks, tk), :HEAD_DIM]
        k1 = k_ref[pl.ds(ks, tk), HEAD_DIM:]
        v = v_ref[pl.ds(ks, tk), :]
        bias = -slope * jnp.abs(rel0 + (qi * tq - j * tk).astype(F32))
        s0 = lax.dot_general(q0, k0, dn, preferred_element_type=F32) * SCALE + bias
        s1 = lax.dot_general(q1, k1, dn, preferred_element_type=F32) * SCALE + bias
        m0, l0, a0 = online(s0, v, m0, l0, a0)
        m1, l1, a1 = online(s1, v, m1, l1, a1)
        return m0, l0, a0, m1, l1, a1

    def init():
        return (jnp.full((tq, 1), MASK_VALUE, F32), jnp.zeros((tq, 1), F32),
                jnp.zeros((tq, 2 * HEAD_DIM), F32))

    _, l0, a0, _, l1, a1 = lax.fori_loop(0, n_kv, body, init() + init())
    o = a0 / l0 - lam * (a1 / l1)
    y = o * lax.rsqrt(jnp.mean(o * o, axis=-1, keepdims=True) + EPS) * sub_ref[...]
    o_ref[...] = (y * (1.0 - lambda_init)).astype(o_ref.dtype)


def attention_b(proj, lam_q1, lam_k1, lam_q2, lam_k2, sub_gain, lambda_init, q_col_block):
    b, t, _ = proj.shape
    w = 2 * HEAD_DIM
    tq = _tile(t, 256)
    tk = _tile(t, 512)
    lam = jnp.stack([lam_q1, lam_k1, lam_q2, lam_k2]).astype(F32)
    smem = pl.BlockSpec(memory_space=pltpu.SMEM)
    return pl.pallas_call(
        functools.partial(_attn_b_kernel, tq=tq, tk=tk, n_kv=t // tk, lambda_init=lambda_init),
        grid=(b, B_HEADS, t // tq),
        in_specs=[smem,
                  pl.BlockSpec((4, HEAD_DIM), lambda bi, h, i: (0, 0)),
                  pl.BlockSpec((1, w), lambda bi, h, i: (0, 0)),
                  pl.BlockSpec((None, tq, w), lambda bi, h, i: (bi, i, q_col_block + h)),
                  pl.BlockSpec((None, t, w), lambda bi, h, i: (bi, 0, q_col_block + B_HEADS + h)),
                  pl.BlockSpec((None, t, w), lambda bi, h, i: (bi, 0, q_col_block + 2 * B_HEADS + h))],
        out_specs=pl.BlockSpec((None, tq, w), lambda bi, h, i: (bi, i, h)),
        out_shape=jax.ShapeDtypeStruct((b, t, B_HEADS * w), BF16),
        compiler_params=_cparams("parallel", "parallel", "arbitrary"),
        name="attn_b",
    )(_alibi_slopes(B_HEADS), lam, sub_gain.reshape(1, w), proj, proj, proj)


Q_SUB = 128
RADIUS = 64


def _dilated_kernel(slopes_ref, q_ref, k_ref, v_ref, o_ref, lse_ref, *, length, heads, dilation, win):
    hc = pl.program_id(2)
    rel0 = (lax.broadcasted_iota(jnp.int32, (Q_SUB, win), 0)
            - lax.broadcasted_iota(jnp.int32, (Q_SUB, win), 1))
    dn = (((1,), (1,)), ((), ()))
    for h in range(heads):
        cols = slice(h * HEAD_DIM, (h + 1) * HEAD_DIM)
        slope = slopes_ref[hc * heads + h] * dilation

        def body(sb, carry, cols=cols, slope=slope):
            q0 = pl.multiple_of(sb * Q_SUB, Q_SUB)
            start = pl.multiple_of(jnp.clip(sb * Q_SUB - RADIUS, 0, length - win), RADIUS)
            q = q_ref[pl.ds(q0, Q_SUB), cols]
            k = k_ref[pl.ds(start, win), cols]
            v = v_ref[pl.ds(start, win), cols]
            rel = jnp.abs(rel0 + (q0 - start))
            s = lax.dot_general(q, k, dn, preferred_element_type=F32) * SCALE - slope * rel.astype(F32)
            s = jnp.where(rel <= RADIUS, s, MASK_VALUE)
            m = jnp.max(s, axis=-1, keepdims=True)
            e = jnp.exp(s - m)
            den = jnp.sum(e, axis=-1, keepdims=True)
            p = (e / den).astype(BF16)
            o_ref[pl.ds(q0, Q_SUB), cols] = jnp.dot(p, v, preferred_element_type=F32).astype(o_ref.dtype)
            lse_ref[pl.ds(q0, Q_SUB), cols] = jnp.broadcast_to(m + jnp.log(den), (Q_SUB, HEAD_DIM))
            return carry

        lax.fori_loop(0, length // Q_SUB, body, 0)


def dilated_group(proj, group, dilation):
    b, t, c_all = proj.shape
    r = dilation
    length = t // r
    hd = C_HEADS * HEAD_DIM
    win = min(2 * Q_SUB, length)
    cw = min(hd, max(HEAD_DIM, (1 << 19) // length))
    nhc = hd // cw
    in_blocks = c_all // cw
    pv = proj.reshape(b, length, r * c_all)

    def in_spec(j):
        base = (group * 3 + j) * nhc
        return pl.BlockSpec((None, length, cw), lambda bi, c, hc: (bi, 0, c * in_blocks + base + hc))

    out_spec = pl.BlockSpec((None, length, cw), lambda bi, c, hc: (bi, 0, c * nhc + hc))
    o, lse = pl.pallas_call(
        functools.partial(_dilated_kernel, length=length, heads=cw // HEAD_DIM, dilation=float(r), win=win),
        grid=(b, r, nhc),
        in_specs=[pl.BlockSpec(memory_space=pltpu.SMEM), in_spec(0), in_spec(1), in_spec(2)],
        out_specs=[out_spec, out_spec],
        out_shape=[jax.ShapeDtypeStruct((b, length, r * hd), BF16),
                   jax.ShapeDtypeStruct((b, length, r * hd), F32)],
        compiler_params=_cparams("parallel", "parallel", "arbitrary"),
        name=f"dilated_r{r}",
    )(_alibi_slopes(C_HEADS), pv, pv, pv)
    return o.reshape(b * t, hd), lse.reshape(b * t, hd)


def _combine_kernel(o1_ref, o2_ref, o3_ref, l1_ref, l2_ref, l3_ref, o_ref):
    l1, l2, l3 = l1_ref[...], l2_ref[...], l3_ref[...]
    mx = jnp.maximum(jnp.maximum(l1, l2), l3)
    e1, e2, e3 = jnp.exp(l1 - mx), jnp.exp(l2 - mx), jnp.exp(l3 - mx)
    num = (e1 * o1_ref[...].astype(F32) + e2 * o2_ref[...].astype(F32) + e3 * o3_ref[...].astype(F32))
    o_ref[...] = (num / (e1 + e2 + e3)).astype(o_ref.dtype)


def combine_groups(outs, lses):
    m, d = outs[0].shape
    tm = _tile(m, 256)
    row = pl.BlockSpec((tm, d), lambda i: (i, 0))
    return pl.pallas_call(
        _combine_kernel, grid=(m // tm,), in_specs=[row] * 6, out_specs=row,
        out_shape=jax.ShapeDtypeStruct((m, d), BF16),
        compiler_params=_cparams("parallel"), name="combine",
    )(*outs, *lses)


def _conv_gate_kernel(g_ref, u_ref, wg_ref, wu_ref, bg_ref, bu_ref, o_ref, *, t, rc):
    n_chunks = t // rc
    row = lax.broadcasted_iota(jnp.int32, (rc, 1), 0)
    is_first = row == 0
    is_last = row == rc - 1
    wg = wg_ref[...]
    wu = wu_ref[...]

    def conv(ref, w, bias, i, r0):
        x = ref[pl.ds(r0, rc), :].astype(F32)
        lo = pl.multiple_of(jnp.maximum(r0 - 16, 0), 16)
        hi = pl.multiple_of(jnp.minimum(r0 + rc, t - 16), 16)
        before = jnp.where(i > 0, ref[pl.ds(lo, 16), :][15:16, :].astype(F32), 0.0)
        after = jnp.where(i < n_chunks - 1, ref[pl.ds(hi, 16), :][0:1, :].astype(F32), 0.0)
        prev = jnp.where(is_first, before, pltpu.roll(x, 1, 0))
        nxt = jnp.where(is_last, after, pltpu.roll(x, rc - 1, 0))
        return prev * w[0:1] + x * w[1:2] + nxt * w[2:3] + bias

    def body(i, carry):
        r0 = pl.multiple_of(i * rc, rc)
        g = conv(g_ref, wg, bg_ref[...], i, r0)
        u = conv(u_ref, wu, bu_ref[...], i, r0)
        act = g * (1.0 / (1.0 + jnp.exp(-g))) * u
        o_ref[pl.ds(r0, rc), :] = act.astype(o_ref.dtype)
        return carry

    lax.fori_loop(0, n_chunks, body, 0)


def conv_gate(hu, conv_w, conv_b):
    b, t, f2 = hu.shape
    f = f2 // 2
    tc = _tile(f, 256)
    nf = f // tc
    rc = _tile(t, 256)
    cw = conv_w.astype(F32)
    cb = conv_b.astype(F32).reshape(1, f2)
    seq_g = pl.BlockSpec((None, t, tc), lambda bi, j: (bi, 0, j))
    seq_u = pl.BlockSpec((None, t, tc), lambda bi, j: (bi, 0, nf + j))
    w_g = pl.BlockSpec((CONV_W, tc), lambda bi, j: (0, j))
    w_u = pl.BlockSpec((CONV_W, tc), lambda bi, j: (0, nf + j))
    b_g = pl.BlockSpec((1, tc), lambda bi, j: (0, j))
    b_u = pl.BlockSpec((1, tc), lambda bi, j: (0, nf + j))
    return pl.pallas_call(
        functools.partial(_conv_gate_kernel, t=t, rc=rc),
        grid=(b, nf),
        in_specs=[seq_g, seq_u, w_g, w_u, b_g, b_u],
        out_specs=pl.BlockSpec((None, t, tc), lambda bi, j: (bi, 0, j)),
        out_shape=jax.ShapeDtypeStruct((b, t, f), BF16),
        compiler_params=_cparams("parallel", "parallel"),
        name="conv_gate",
    )(hu, hu, cw, cw, cb, cb)


def _conv_ffn(xn, b, t, w_up, conv_w, conv_b, w_down):
    hu = matmul(xn, w_up, 1024, 512)
    act = conv_gate(hu.reshape(b, t, -1), conv_w, conv_b)
    return matmul(act.reshape(b * t, -1), w_down, 512, 512)


def _trunk(x, p):
    b, t, d = x.shape
    x2 = x.reshape(b * t, d)

    xn = rmsnorm_bf16(x2, p["l0_norm_pre_mix"])
    proj = matmul(xn, p["l0_w_in"], 1024, 1024).reshape(b, t, -1)
    qk = qk_prep(proj, p["l0_q_norm"], p["l0_k_norm"])
    oa = attention_a(qk, proj, A_HEADS + A_KV_HEADS)
    lambda_init = 0.8 - 0.6 * math.exp(-0.3 * 0)
    ob = attention_b(proj, p["l0_lam_q1"], p["l0_lam_k1"], p["l0_lam_q2"], p["l0_lam_k2"],
                     p["l0_subln"], lambda_init, (A_HEADS + 2 * A_KV_HEADS) // 2)
    ka = A_HEADS * HEAD_DIM
    h = matmul2(oa.reshape(b * t, -1), ob.reshape(b * t, -1),
                p["l0_w_out"][:ka], p["l0_w_out"][ka:], 1024, 1024)
    x2, xn = post_norm_residual(h, x2, p["l0_norm_post_mix"], p["l0_norm_pre_ffn"])
    h = _conv_ffn(xn, b, t, p["l0_ffn_w_up"], p["l0_ffn_conv_w"], p["l0_ffn_conv_b"], p["l0_ffn_w_down"])
    x2, xn = post_norm_residual(h, x2, p["l0_norm_post_ffn"], p["l1_norm_pre_mix"])

    proj = matmul(xn, p["l1_w_in"], 1024, 1024).reshape(b, t, -1)
    outs, lses = zip(*[dilated_group(proj, g, r) for g, (_, r) in enumerate(C_GROUPS)])
    o = combine_groups(outs, lses)
    h = matmul(o, p["l1_w_out"], 1024, 1024)
    x2, xn = post_norm_residual(h, x2, p["l1_norm_post_mix"], p["l1_norm_pre_ffn"])
    h = _conv_ffn(xn, b, t, p["l1_ffn_w_up"], p["l1_ffn_conv_w"], p["l1_ffn_conv_b"], p["l1_ffn_w_down"])
    x2 = post_norm_residual(h, x2, p["l1_norm_post_ffn"])
    return x2.reshape(b, t, d)


_MATMUL_WEIGHTS = ("l0_w_in", "l0_w_out", "l0_ffn_w_up", "l0_ffn_w_down",
                   "l1_w_in", "l1_w_out", "l1_ffn_w_up", "l1_ffn_w_down")


def kernel(x_prompt, x_sample, l0_norm_pre_mix, l0_w_in, l0_q_norm, l0_k_norm, l0_lam_q1, l0_lam_k1, l0_lam_q2, l0_lam_k2, l0_subln, l0_w_out, l0_norm_post_mix, l0_norm_pre_ffn, l0_ffn_w_up, l0_ffn_conv_w, l0_ffn_conv_b, l0_ffn_w_down, l0_norm_post_ffn, l1_norm_pre_mix, l1_w_in, l1_w_out, l1_norm_post_mix, l1_norm_pre_ffn, l1_ffn_w_up, l1_ffn_conv_w, l1_ffn_conv_b, l1_ffn_w_down, l1_norm_post_ffn):
    p = dict(locals())
    p.pop("x_prompt")
    p.pop("x_sample")
    for name in _MATMUL_WEIGHTS:
        p[name] = p[name].astype(BF16)
    return _trunk(x_prompt, p), _trunk(x_sample, p)
```
